```python
import jax, jax.numpy as jnp
from jax import lax
import numpy as np

D_MODEL = 2048
BATCH = 4
SEQ = 4096
DEPTH = 2

N_MIXERS = 2
N_FOURIER_LAYERS = (DEPTH + 1) // 2
N_ATTN_LAYERS = DEPTH // 2
FOURIER_GROUPS = 8
FOURIER_GROUP_WIDTH = D_MODEL // FOURIER_GROUPS
HEAD_DIM = 64
N_Q_HEADS = D_MODEL // HEAD_DIM
N_KV_HEADS = N_Q_HEADS // 4
Q_PER_KV = N_Q_HEADS // N_KV_HEADS
QKV_WIDTH = (N_Q_HEADS + 2 * N_KV_HEADS) * HEAD_DIM
WINDOW = 128
BLOCK = 128
ROPE_THETA = 10000.0
PEER_HEADS = 8
N_KEYS = 128
N_EXPERTS = N_KEYS * N_KEYS
PEER_QUERY_DIM = 256
PEER_HALF = PEER_QUERY_DIM // 2
PEER_TOPK = 16
PEER_CHUNK = 128
EPS = 1e-6
NEG_INF = -1e30

kernel_name = "hybrid_fnet_swa_peer_encoder"


def rmsnorm(x, g):
    xf = x.astype(jnp.float32)
    y = xf * lax.rsqrt(jnp.mean(xf * xf, axis=-1, keepdims=True) + EPS)
    return (y * g.astype(jnp.float32)).astype(x.dtype)


def fourier_mixer(h, w_o):
    B, S, D = h.shape
    hg = h.astype(jnp.float32).reshape(B, S, FOURIER_GROUPS, FOURIER_GROUP_WIDTH)
    hg = jnp.transpose(hg, (0, 2, 1, 3))
    y = jnp.fft.fft2(hg, axes=(-2, -1), norm='ortho').real
    y = jnp.transpose(y, (0, 2, 1, 3)).reshape(B, S, D).astype(h.dtype)
    return y @ w_o


def rope(x, pos):
    hd = x.shape[-1]
    inv_freq = ROPE_THETA ** (-jnp.arange(0, hd, 2, dtype=jnp.float32) / hd)
    ang = pos[:, None] * inv_freq[None, :]
    cos = jnp.cos(ang)[None, :, None, :]
    sin = jnp.sin(ang)[None, :, None, :]
    xf = x.astype(jnp.float32)
    x1, x2 = xf[..., : hd // 2], xf[..., hd // 2:]
    out = jnp.concatenate([x1 * cos - x2 * sin, x2 * cos + x1 * sin], axis=-1)
    return out.astype(x.dtype)


def _band_windows(t, nb):
    B, S, KV, hd = t.shape
    tp = jnp.pad(t, ((0, 0), (BLOCK, BLOCK), (0, 0), (0, 0))).reshape(B, nb + 2, BLOCK, KV, hd)
    return jnp.concatenate([tp[:, :-2], tp[:, 1:-1], tp[:, 2:]], axis=2)


def windowed_gqa_mixer(h, w_qkv, w_o, sinks):
    B, S, D = h.shape
    nb = S // BLOCK
    qkv = h @ w_qkv
    q = qkv[..., : N_Q_HEADS * HEAD_DIM].reshape(B, S, N_Q_HEADS, HEAD_DIM)
    k = qkv[..., N_Q_HEADS * HEAD_DIM:(N_Q_HEADS + N_KV_HEADS) * HEAD_DIM].reshape(B, S, N_KV_HEADS, HEAD_DIM)
    v = qkv[..., (N_Q_HEADS + N_KV_HEADS) * HEAD_DIM:].reshape(B, S, N_KV_HEADS, HEAD_DIM)
    pos = jnp.arange(S, dtype=jnp.float32)
    q = rope(q, pos)
    k = rope(k, pos)
    qb = q.reshape(B, nb, BLOCK, N_KV_HEADS, Q_PER_KV, HEAD_DIM)
    kw = _band_windows(k, nb)
    vw = _band_windows(v, nb)
    scale = HEAD_DIM ** -0.5
    scores = jnp.einsum('bnqkgd,bnskd->bnkgqs', qb, kw).astype(jnp.float32) * scale
    qi = jnp.arange(BLOCK)[:, None]
    si = jnp.arange(3 * BLOCK)[None, :]
    band = jnp.abs(si - BLOCK - qi) <= WINDOW
    key_pos = jnp.arange(nb)[:, None] * BLOCK - BLOCK + jnp.arange(3 * BLOCK)[None, :]
    inside = (key_pos >= 0) & (key_pos < S)
    mask = band[None, :, :] & inside[:, None, :]
    scores = jnp.where(mask[None, :, None, None, :, :], scores, NEG_INF)
    sink = sinks.astype(jnp.float32).reshape(N_KV_HEADS, Q_PER_KV)[None, None, :, :, None, None]
    m = jnp.maximum(jnp.max(scores, axis=-1, keepdims=True), sink)
    p = jnp.exp(scores - m)
    p = p / (jnp.sum(p, axis=-1, keepdims=True) + jnp.exp(sink - m))
    out = jnp.einsum('bnkgqs,bnskd->bnqkgd', p.astype(vw.dtype), vw).reshape(B, S, D)
    return out @ w_o


def peer_mixer(h, w_q, sub_keys, u, v):
    B, S, D = h.shape
    T = B * S

    def chunk_fn(xc):
        C = xc.shape[0]
        q = (xc @ w_q).reshape(C, PEER_HEADS, 2, PEER_HALF)
        s = jnp.einsum('chpd,hpkd->chpk', q, sub_keys).astype(jnp.float32)
        s1, i1 = lax.top_k(s[:, :, 0], PEER_TOPK)
        s2, i2 = lax.top_k(s[:, :, 1], PEER_TOPK)
        cand = (s1[..., :, None] + s2[..., None, :]).reshape(C, PEER_HEADS, PEER_TOPK * PEER_TOPK)
        cidx = (i1[..., :, None] * N_KEYS + i2[..., None, :]).reshape(C, PEER_HEADS, PEER_TOPK * PEER_TOPK)
        top, sel = lax.top_k(cand, PEER_TOPK)
        eidx = jnp.take_along_axis(cidx, sel, axis=-1)
        g = jax.nn.softmax(top, axis=-1)
        u_sel = jnp.take(u, eidx, axis=0)
        a = jax.nn.gelu(jnp.einsum('chkd,cd->chk', u_sel, xc).astype(jnp.float32))
        v_sel = jnp.take(v, eidx, axis=0)
        return jnp.einsum('chk,chkd->cd', (g * a).astype(xc.dtype), v_sel)

    out = lax.map(chunk_fn, h.reshape(T // PEER_CHUNK, PEER_CHUNK, D))
    return out.reshape(B, S, D)


def setup_inputs(seed: int = 0) -> dict:
    key = jax.random.key(seed)
    ks = jax.random.split(key, 13)
    f32 = jnp.float32
    sd = D_MODEL ** -0.5
    x = jax.random.normal(ks[0], (BATCH, SEQ, D_MODEL), f32)
    mix_norm = 1.0 + 0.02 * jax.random.normal(ks[1], (DEPTH, D_MODEL), f32)
    ffn_norm = 1.0 + 0.02 * jax.random.normal(ks[2], (DEPTH, D_MODEL), f32)
    fourier_w_o = jax.random.normal(ks[3], (N_FOURIER_LAYERS, D_MODEL, D_MODEL), f32) * sd
    attn_w_qkv = jax.random.normal(ks[4], (N_ATTN_LAYERS, D_MODEL, QKV_WIDTH), f32) * sd
    attn_w_o = jax.random.normal(ks[5], (N_ATTN_LAYERS, D_MODEL, D_MODEL), f32) * sd
    attn_sinks = 0.5 * jax.random.normal(ks[6], (N_ATTN_LAYERS, N_Q_HEADS), f32)
    peer_w_q = jax.random.normal(ks[7], (DEPTH, D_MODEL, PEER_HEADS * PEER_QUERY_DIM), f32) * sd
    peer_sub_keys = jax.random.normal(ks[8], (DEPTH, PEER_HEADS, 2, N_KEYS, PEER_HALF), f32) * PEER_HALF ** -0.5
    peer_u = jax.random.normal(ks[9], (DEPTH, N_EXPERTS, D_MODEL), f32) * sd
    peer_v = jax.random.normal(ks[10], (DEPTH, N_EXPERTS, D_MODEL), f32) * (PEER_HEADS * PEER_TOPK) ** -0.5
    final_norm = 1.0 + 0.02 * jax.random.normal(ks[11], (D_MODEL,), f32)
    return {"x": x, "mix_norm": mix_norm, "ffn_norm": ffn_norm, "fourier_w_o": fourier_w_o,
            "attn_w_qkv": attn_w_qkv, "attn_w_o": attn_w_o, "attn_sinks": attn_sinks,
            "peer_w_q": peer_w_q, "peer_sub_keys": peer_sub_keys, "peer_u": peer_u,
            "peer_v": peer_v, "final_norm": final_norm}


def reference(x, mix_norm, ffn_norm, fourier_w_o, attn_w_qkv, attn_w_o, attn_sinks,
              peer_w_q, peer_sub_keys, peer_u, peer_v, final_norm):
    for i in range(DEPTH):
        h = rmsnorm(x, mix_norm[i])
        j = i // N_MIXERS
        if i % N_MIXERS == 0:
            x = x + fourier_mixer(h, fourier_w_o[j])
        else:
            x = x + windowed_gqa_mixer(h, attn_w_qkv[j], attn_w_o[j], attn_sinks[j])
        h = rmsnorm(x, ffn_norm[i])
        x = x + peer_mixer(h, peer_w_q[i], peer_sub_keys[i], peer_u[i], peer_v[i])
    return rmsnorm(x, final_norm)
```

```python
import functools
import math

import jax
import jax.numpy as jnp
import numpy as np
from jax import lax
from jax.experimental import pallas as pl
from jax.experimental.pallas import tpu as pltpu

F32 = jnp.float32
BF16 = jnp.bfloat16

EPS = 1e-6
NEG_INF = -1e30

FOURIER_GROUPS = 8
HEAD_DIM = 64
Q_PER_KV = 4
WINDOW = 128
ROPE_THETA = 10000.0
PEER_HEADS = 8
N_KEYS = 128
PEER_TOPK = 16

LANES = 128
VMEM_LIMIT = 56 * 1024 * 1024


def _params(sem, vmem=VMEM_LIMIT):
    return pltpu.CompilerParams(dimension_semantics=sem, vmem_limit_bytes=vmem)


def _rmsnorm_kernel(x_ref, g_ref, o_ref):
    x = x_ref[...]
    ms = jnp.mean(x * x, axis=-1, keepdims=True)
    o_ref[...] = (x * lax.rsqrt(ms + EPS) * g_ref[...]).astype(o_ref.dtype)


def rmsnorm(x2d, g, out_dtype, tr=512):
    t, d = x2d.shape
    tr = min(tr, t)
    return pl.pallas_call(
        _rmsnorm_kernel,
        grid=(t // tr,),
        in_specs=[pl.BlockSpec((tr, d), lambda i: (i, 0)),
                  pl.BlockSpec((1, d), lambda i: (0, 0))],
        out_specs=pl.BlockSpec((tr, d), lambda i: (i, 0)),
        out_shape=jax.ShapeDtypeStruct((t, d), out_dtype),
        compiler_params=_params(("parallel",)),
        name="rmsnorm",
    )(x2d, g.reshape(1, d).astype(F32))


def _mm_kernel(a_ref, b_ref, o_ref):
    o_ref[...] = jnp.dot(a_ref[...], b_ref[...], preferred_element_type=F32).astype(o_ref.dtype)


def _mm_res_kernel(a_ref, b_ref, r_ref, o_ref):
    o_ref[...] = (r_ref[...] + jnp.dot(a_ref[...], b_ref[...], preferred_element_type=F32)).astype(o_ref.dtype)


def matmul(a, b, out_dtype, residual=None, tm=1024, tn=1024, name="matmul"):
    m, k = a.shape
    _, n = b.shape
    tm, tn = min(tm, m), min(tn, n)
    in_specs = [pl.BlockSpec((tm, k), lambda j, i: (i, 0)),
                pl.BlockSpec((k, tn), lambda j, i: (0, j))]
    args = [a, b]
    kern = _mm_kernel
    if residual is not None:
        in_specs.append(pl.BlockSpec((tm, tn), lambda j, i: (i, j)))
        args.append(residual)
        kern = _mm_res_kernel
    return pl.pallas_call(
        kern,
        grid=(n // tn, m // tm),
        in_specs=in_specs,
        out_specs=pl.BlockSpec((tm, tn), lambda j, i: (i, j)),
        out_shape=jax.ShapeDtypeStruct((m, n), out_dtype),
        compiler_params=_params(("parallel", "parallel")),
        name=name,
    )(*args)


def _bmm_acc_kernel(a_ref, b_ref, o_ref, acc_ref):
    kk = pl.program_id(3)

    @pl.when(kk == 0)
    def _():
        acc_ref[...] = jnp.zeros_like(acc_ref)

    acc_ref[...] += jnp.dot(a_ref[...], b_ref[...], preferred_element_type=F32)

    @pl.when(kk == pl.num_programs(3) - 1)
    def _():
        o_ref[...] = acc_ref[...].astype(o_ref.dtype)


def shared_lhs_bmm(a, b, out_dtype, tm=1024, tn=2048, tk=1024):
    m, k = a.shape
    nb, _, n = b.shape
    tm, tn, tk = min(tm, m), min(tn, n), min(tk, k)
    return pl.pallas_call(
        _bmm_acc_kernel,
        grid=(nb, m // tm, n // tn, k // tk),
        in_specs=[pl.BlockSpec((tm, tk), lambda bi, i, j, kk: (i, kk)),
                  pl.BlockSpec((None, tk, tn), lambda bi, i, j, kk: (bi, kk, j))],
        out_specs=pl.BlockSpec((None, tm, tn), lambda bi, i, j, kk: (bi, i, j)),
        out_shape=jax.ShapeDtypeStruct((nb, m, n), out_dtype),
        scratch_shapes=[pltpu.VMEM((tm, tn), F32)],
        compiler_params=_params(("parallel", "parallel", "parallel", "arbitrary")),
        name="seq_dft",
    )(a, b)


def _channel_dft_kernel(h_ref, cs_ref, z_ref, *, groups, width):
    cs = cs_ref[...]
    for g in range(groups):
        r = jnp.dot(h_ref[:, g * width:(g + 1) * width], cs, preferred_element_type=F32)
        z_ref[0, :, g * width:(g + 1) * width] = r[:, :width].astype(z_ref.dtype)
        z_ref[1, :, g * width:(g + 1) * width] = r[:, width:].astype(z_ref.dtype)


def _channel_tables(width):
    c = np.arange(width)
    ang = 2.0 * np.pi * ((c[:, None] * c[None, :]) % width) / width
    scale = 1.0 / math.sqrt(width)
    return np.concatenate([np.cos(ang), np.sin(ang)], axis=1) * scale


def _seq_dft_matrix(s):
    s1n = 1 << (int(math.log2(s)) // 2)
    s2n = s // s1n
    k = jnp.arange(s, dtype=jnp.int32)[:, None]
    s1 = jnp.arange(s1n, dtype=jnp.int32)[None, :]
    s2 = jnp.arange(s2n, dtype=jnp.int32)[None, :]
    w = 2.0 * math.pi / s
    ang_a = ((k * s1 * s2n) % s).astype(F32) * w
    ang_b = ((k * s2) % s).astype(F32) * w
    ca, sa = jnp.cos(ang_a)[:, :, None], jnp.sin(ang_a)[:, :, None]
    cb, sb = jnp.cos(ang_b)[:, None, :], jnp.sin(ang_b)[:, None, :]
    scale = 1.0 / math.sqrt(s)
    cos = ((ca * cb - sa * sb) * scale).reshape(s, s)
    sin = ((sa * cb + ca * sb) * scale).reshape(s, s)
    return jnp.concatenate([cos, -sin], axis=1).astype(BF16)


def fourier_mixer(h3, x2d, w_o, groups=FOURIER_GROUPS, tt=512):
    b, s, d = h3.shape
    width = d // groups
    tt = min(tt, s)
    cs = jnp.asarray(_channel_tables(width), dtype=BF16)
    z = pl.pallas_call(
        functools.partial(_channel_dft_kernel, groups=groups, width=width),
        grid=(b, s // tt),
        in_specs=[pl.BlockSpec((None, tt, d), lambda bi, si: (bi, si, 0)),
                  pl.BlockSpec((width, 2 * width), lambda bi, si: (0, 0))],
        out_specs=pl.BlockSpec((None, 2, tt, d), lambda bi, si: (bi, 0, si, 0)),
        out_shape=jax.ShapeDtypeStruct((b, 2, s, d), BF16),
        compiler_params=_params(("parallel", "parallel")),
        name="channel_dft",
    )(h3, cs)
    y = shared_lhs_bmm(_seq_dft_matrix(s), z.reshape(b, 2 * s, d), BF16)
    return matmul(y.reshape(b * s, d), w_o.astype(BF16), F32, residual=x2d, name="fourier_out")


def _rope(x, cos, sin_signed, first_half):
    half = HEAD_DIM // 2
    rot = jnp.where(first_half, pltpu.roll(x, LANES - half, 1), pltpu.roll(x, half, 1))
    return x * cos + rot * sin_signed


def _attn_kernel(sink_ref, q_ref, kp_ref, kc_ref, kn_ref, vp_ref, vc_ref, vn_ref,
                 cq_ref, sq_ref, cp_ref, sp_ref, cn_ref, sn_ref, o_ref, *, n_q_heads, n_kv_heads, blk):
    n = pl.program_id(1)
    nb = pl.num_programs(1)
    hd = HEAD_DIM
    scale = hd ** -0.5
    lane = lax.broadcasted_iota(jnp.int32, (blk, LANES), 1)
    first_half = (lane % hd) < (hd // 2)

    def rope_slab(ref, c, cos, sin):
        return _rope(ref[:, c * LANES:(c + 1) * LANES].astype(F32), cos, sin, first_half)

    cq, sq = cq_ref[...], sq_ref[...]
    cos_w = (cp_ref[...], cq, cn_ref[...])
    sin_w = (sp_ref[...], sq, sn_ref[...])
    k_refs = (kp_ref, kc_ref, kn_ref)
    v_refs = (vp_ref, vc_ref, vn_ref)

    qi = lax.broadcasted_iota(jnp.int32, (blk, 3 * blk), 0)
    si = lax.broadcasted_iota(jnp.int32, (blk, 3 * blk), 1)
    key_pos = n * blk - blk + si
    mask = (jnp.abs(si - blk - qi) <= WINDOW) & (key_pos >= 0) & (key_pos < nb * blk)

    kv_per_slab = LANES // hd
    for c in range(n_kv_heads // kv_per_slab):
        k_slab = jnp.concatenate(
            [rope_slab(k_refs[w], c, cos_w[w], sin_w[w]) for w in range(3)], axis=0).astype(BF16)
        v_slab = jnp.concatenate([v_refs[w][:, c * LANES:(c + 1) * LANES] for w in range(3)], axis=0)
        for gi in range(kv_per_slab):
            g = c * kv_per_slab + gi
            kg = k_slab[:, gi * hd:(gi + 1) * hd]
            vg = v_slab[:, gi * hd:(gi + 1) * hd]
            for qc in range(Q_PER_KV * hd // LANES):
                col = (g * Q_PER_KV * hd) // LANES + qc
                q_slab = (rope_slab(q_ref, col, cq, sq) * scale).astype(BF16)
                for qh in range(LANES // hd):
                    head = col * (LANES // hd) + qh
                    q = q_slab[:, qh * hd:(qh + 1) * hd]
                    s = lax.dot_general(q, kg, (((1,), (1,)), ((), ())), preferred_element_type=F32)
                    s = jnp.where(mask, s, NEG_INF)
                    sink = sink_ref[head]
                    m = jnp.maximum(jnp.max(s, axis=-1, keepdims=True), sink)
                    p = jnp.exp(s - m)
                    denom = jnp.sum(p, axis=-1, keepdims=True) + jnp.exp(sink - m)
                    o = jnp.dot(p.astype(BF16), vg, preferred_element_type=F32) / denom
                    o_ref[:, head * hd:(head + 1) * hd] = o.astype(o_ref.dtype)


def windowed_attention(qkv3, sinks, blk=WINDOW):
    b, s, width = qkv3.shape
    hd = HEAD_DIM
    n_kv = width // ((Q_PER_KV + 2) * hd)
    n_q = n_kv * Q_PER_KV
    dq, dkv = n_q * hd, n_kv * hd
    nb = s // blk
    assert dq % dkv == 0 and dkv % LANES == 0
    kcol, vcol = dq // dkv, dq // dkv + 1

    pos = jnp.arange(s, dtype=F32)
    inv_freq = ROPE_THETA ** (-jnp.arange(0, hd, 2, dtype=F32) / hd)
    ang = pos[:, None] * inv_freq[None, :]
    cos, sin = jnp.cos(ang), jnp.sin(ang)
    cos_t = jnp.tile(jnp.concatenate([cos, cos], axis=1), (1, LANES // hd))
    sin_t = jnp.tile(jnp.concatenate([-sin, sin], axis=1), (1, LANES // hd))

    prev = lambda bi, n: (bi, jnp.maximum(n - 1, 0))
    cur = lambda bi, n: (bi, n)
    nxt = lambda bi, n: (bi, jnp.minimum(n + 1, nb - 1))

    def kv_spec(rows, col):
        return pl.BlockSpec((None, blk, dkv), lambda bi, n: rows(bi, n) + (col,))

    def tab_spec(rows):
        return pl.BlockSpec((blk, LANES), lambda bi, n: (rows(bi, n)[1], 0))

    return pl.pallas_call(
        functools.partial(_attn_kernel, n_q_heads=n_q, n_kv_heads=n_kv, blk=blk),
        grid=(b, nb),
        in_specs=[pl.BlockSpec(memory_space=pltpu.SMEM),
                  pl.BlockSpec((None, blk, dq), lambda bi, n: (bi, n, 0)),
                  kv_spec(prev, kcol), kv_spec(cur, kcol), kv_spec(nxt, kcol),
                  kv_spec(prev, vcol), kv_spec(cur, vcol), kv_spec(nxt, vcol),
                  tab_spec(cur), tab_spec(cur), tab_spec(prev), tab_spec(prev), tab_spec(nxt), tab_spec(nxt)],
        out_specs=pl.BlockSpec((None, blk, dq), lambda bi, n: (bi, n, 0)),
        out_shape=jax.ShapeDtypeStruct((b, s, dq), BF16),
        compiler_params=_params(("parallel", "parallel")),
        name="window_attn",
    )(sinks.astype(F32), qkv3, qkv3, qkv3, qkv3, qkv3, qkv3, qkv3,
      cos_t, sin_t, cos_t, sin_t, cos_t, sin_t)


def _top_rows(x, count):
    rows = []
    for _ in range(count):
        m = jnp.max(x, axis=0, keepdims=True)
        rows.append(m)
        x = jnp.where(x == m, NEG_INF, x)
    return rows


def _stack_rows(rows, n_rows):
    n = rows[0].shape[1]
    idx = lax.broadcasted_iota(jnp.int32, (n_rows, n), 0)
    out = jnp.full((n_rows, n), NEG_INF, F32)
    for r, row in enumerate(rows):
        out = jnp.where(idx == r, row, out)
    return out


def _pair_sum_candidates(a, b, n_top):
    sub = 8
    assert sub < n_top <= 3 * sub and (sub + 1) * 2 > n_top
    n_rows = 3 * sub
    arr_a, arr_b = _stack_rows(a, n_rows), _stack_rows(b, n_rows)
    idx = lax.broadcasted_iota(jnp.int32, (sub, a[0].shape[1]), 0)
    blocks = []
    for r in range(sub):
        limit = n_top // (r + 1)
        blocks.append(jnp.where(idx < limit, a[r] + arr_b[:sub], NEG_INF))
    blocks.append(arr_a[sub:] + b[0])
    blocks.append(arr_b[sub:] + a[0])
    return jnp.concatenate(blocks, axis=0)


def _peer_select_kernel(h_ref, wq_ref, keys_ref, s2_ref, th_ref, p1_ref, p2_ref, *, heads, topk):
    q = jnp.dot(h_ref[...], wq_ref[...], preferred_element_type=F32).astype(BF16)
    half = keys_ref.shape[-1]
    nt = (((1,), (1,)), ((), ()))
    n_top = topk + 1
    for h in range(heads):
        q1 = q[:, (2 * h) * half:(2 * h + 1) * half]
        q2 = q[:, (2 * h + 1) * half:(2 * h + 2) * half]
        s1 = lax.dot_general(keys_ref[h, 0], q1, nt, preferred_element_type=F32)
        s2 = lax.dot_general(keys_ref[h, 1], q2, nt, preferred_element_type=F32)
        a = _top_rows(s1, n_top)
        b = _top_rows(s2, n_top)
        top = _top_rows(_pair_sum_candidates(a, b, n_top), n_top)
        tau = 0.5 * (top[topk - 1] + top[topk])
        z = jnp.ones_like(tau)
        for r in range(1, topk):
            z = z + jnp.exp(top[r] - top[0])
        s2_ref[h] = s2
        th_ref[h] = tau - s1
        p1_ref[h] = jnp.exp(s1 - a[0]) / z
        p2_ref[h] = jnp.exp(s2 - b[0])


def peer_select(h2d, w_q, sub_keys, tt=256):
    t, d = h2d.shape
    heads, _, n_keys, half = sub_keys.shape
    tt = min(tt, t)
    out = jax.ShapeDtypeStruct((heads, n_keys, t), F32)
    ospec = pl.BlockSpec((heads, n_keys, tt), lambda i: (0, 0, i))
    return pl.pallas_call(
        functools.partial(_peer_select_kernel, heads=heads, topk=PEER_TOPK),
        grid=(t // tt,),
        in_specs=[pl.BlockSpec((tt, d), lambda i: (i, 0)),
                  pl.BlockSpec(w_q.shape, lambda i: (0, 0)),
                  pl.BlockSpec(sub_keys.shape, lambda i: (0, 0, 0, 0))],
        out_specs=[ospec] * 4,
        out_shape=[out] * 4,
        compiler_params=_params(("parallel",)),
        name="peer_select",
    )(h2d, w_q.astype(BF16), sub_keys.astype(BF16))


def _peer_expert_kernel(h_ref, x_ref, s2_ref, p2_ref, th_ref, p1_ref, u_ref, v_ref, o_ref,
                        a_ref, g_ref, *, heads, n_keys):
    e = pl.program_id(1)
    ec, tt = a_ref.shape

    @pl.when(e == 0)
    def _():
        o_ref[...] = x_ref[...]

    a_ref[...] = lax.dot_general(u_ref[...], h_ref[...], (((1,), (1,)), ((), ())),
                                 preferred_element_type=F32)
    for i in range(ec // n_keys):
        for tb in range(tt // LANES):
            lanes = slice(tb * LANES, (tb + 1) * LANES)
            w = jnp.zeros((n_keys, LANES), F32)
            for h in range(heads):
                sel = s2_ref[h, :, lanes] >= th_ref[h, i:i + 1, lanes]
                w = w + jnp.where(sel, p2_ref[h, :, lanes], 0.0) * p1_ref[h, i:i + 1, lanes]
            act = jax.nn.gelu(a_ref[i * n_keys:(i + 1) * n_keys, lanes])
            g_ref[lanes, i * n_keys:(i + 1) * n_keys] = (w * act).T.astype(g_ref.dtype)
    o_ref[...] += jnp.dot(g_ref[...], v_ref[...], preferred_element_type=F32)


def peer_experts(h2d, x2d, sel, u, v, tt=512, ec=1024):
    t, d = h2d.shape
    n_exp = u.shape[0]
    s2, th, p1, p2 = sel
    heads, n_keys, _ = s2.shape
    tt, ec = min(tt, t), min(ec, n_exp)
    rows = ec // n_keys
    assert ec % n_keys == 0 and n_exp == n_keys * n_keys and tt % LANES == 0 and rows % 8 == 0
    full = pl.BlockSpec((heads, n_keys, tt), lambda i, e: (0, 0, i))
    by_row = pl.BlockSpec((heads, rows, tt), lambda i, e: (0, e, i))
    return pl.pallas_call(
        functools.partial(_peer_expert_kernel, heads=heads, n_keys=n_keys),
        grid=(t // tt, n_exp // ec),
        in_specs=[pl.BlockSpec((tt, d), lambda i, e: (i, 0)),
                  pl.BlockSpec((tt, d), lambda i, e: (i, 0)),
                  full, full, by_row, by_row,
                  pl.BlockSpec((ec, d), lambda i, e: (e, 0)),
                  pl.BlockSpec((ec, d), lambda i, e: (e, 0))],
        out_specs=pl.BlockSpec((tt, d), lambda i, e: (i, 0)),
        out_shape=jax.ShapeDtypeStruct((t, d), F32),
        scratch_shapes=[pltpu.VMEM((ec, tt), F32), pltpu.VMEM((tt, ec), BF16)],
        compiler_params=_params(("parallel", "arbitrary")),
        name="peer_experts",
    )(h2d, x2d, s2, p2, th, p1, u, v)


def peer_mixer(x2d, norm_g, w_q, sub_keys, u, v):
    h = rmsnorm(x2d, norm_g, BF16)
    sel = peer_select(h, w_q, sub_keys)
    return peer_experts(h, x2d, sel, u.astype(BF16), v.astype(BF16))


def kernel(x, mix_norm, ffn_norm, fourier_w_o, attn_w_qkv, attn_w_o, attn_sinks, peer_w_q, peer_sub_keys,
           peer_u, peer_v, final_norm):
    b, s, d = x.shape
    depth = mix_norm.shape[0]
    x2d = x.reshape(b * s, d)
    for i in range(depth):
        h = rmsnorm(x2d, mix_norm[i], BF16)
        j = i // 2
        if i % 2 == 0:
            x2d = fourier_mixer(h.reshape(b, s, d), x2d, fourier_w_o[j])
        else:
            qkv = matmul(h, attn_w_qkv[j].astype(BF16), BF16, name="attn_qkv")
            o = windowed_attention(qkv.reshape(b, s, -1), attn_sinks[j])
            x2d = matmul(o.reshape(b * s, d), attn_w_o[j].astype(BF16), F32, residual=x2d, name="attn_out")
        x2d = peer_mixer(x2d, ffn_norm[i], peer_w_q[i], peer_sub_keys[i], peer_u[i], peer_v[i])
    return rmsnorm(x2d, final_norm, F32).reshape(b, s, d)
```
